```python
import math
import jax
import jax.numpy as jnp
from jax import lax
import numpy as np

D_MODEL = 1024
BATCH = 8
SEQ = 4096
DEPTH = 4

N_MIXERS = 4
RMS_EPS = 1e-6
ROPE_THETA = 500000.0
ROPE_FRACTION = 4

A_HEADS = 8
A_HEAD_DIM = D_MODEL // A_HEADS
MOBA_BLOCK = 256
MOBA_TOPK = 3
MOBA_QCHUNK = 16

B_HEADS = 8
B_HEAD_DIM = D_MODEL // (2 * B_HEADS)
B_QBLOCK = 128
SUBLN_EPS = 1e-5

C_WIDTH = D_MODEL
C_BLOCKS = 8
C_BLOCK_DIM = C_WIDTH // C_BLOCKS
C_CONV = 4
LRU_C = 8.0

D_HEADS = 4
D_QK_DIM = D_MODEL // D_HEADS
D_V_DIM = 2 * D_QK_DIM
RET_CHUNK = 128
RET_THETA = 10000.0
GN_EPS = 1e-5

D_FF = 2816
FFN_CONV = 3

kernel_name = "hybrid_moba_diffattn_rglru_retention_trunk"


def rmsnorm(x, g, eps=RMS_EPS):
    xf = x.astype(jnp.float32)
    y = xf * lax.rsqrt(jnp.mean(jnp.square(xf), axis=-1, keepdims=True) + eps)
    return (y * g.astype(jnp.float32)).astype(x.dtype)


def rope_tables(seq, rot_dim, theta):
    inv = theta ** (-jnp.arange(0, rot_dim, 2, dtype=jnp.float32) / rot_dim)
    ang = jnp.arange(seq, dtype=jnp.float32)[:, None] * inv[None, :]
    return jnp.cos(ang), jnp.sin(ang)


def apply_rotary(x, cos, sin):
    half = cos.shape[-1]
    rot = 2 * half
    xf = x.astype(jnp.float32)
    x1, x2, xp = xf[..., :half], xf[..., half:rot], xf[..., rot:]
    c, s = cos[None, :, None, :], sin[None, :, None, :]
    return jnp.concatenate([x1 * c - x2 * s, x2 * c + x1 * s, xp], axis=-1).astype(x.dtype)


def causal_dwconv(x, w, b):
    width = w.shape[0]
    y = lax.conv_general_dilated(
        x, w[:, None, :], window_strides=(1,), padding=[(width - 1, 0)],
        dimension_numbers=('NWC', 'WIO', 'NWC'), feature_group_count=x.shape[-1])
    return y + b


def moba_mixer(x, w_in, w_out, cos, sin):
    B, S, _ = x.shape
    H, Dh = A_HEADS, A_HEAD_DIM
    qkv = (x @ w_in).reshape(B, S, 3, H, Dh)
    q = apply_rotary(qkv[:, :, 0], cos, sin).transpose(0, 2, 1, 3)
    k = apply_rotary(qkv[:, :, 1], cos, sin).transpose(0, 2, 1, 3)
    v = qkv[:, :, 2].transpose(0, 2, 1, 3)
    n_blk = -(-S // MOBA_BLOCK)
    s_pad = n_blk * MOBA_BLOCK
    pad = ((0, 0), (0, 0), (0, s_pad - S), (0, 0))
    q, k, v = jnp.pad(q, pad), jnp.pad(k, pad), jnp.pad(v, pad)
    k_blocks = k.reshape(B, H, n_blk, MOBA_BLOCK, Dh)
    v_blocks = v.reshape(B, H, n_blk, MOBA_BLOCK, Dh)
    k_mean = k_blocks.astype(jnp.float32).mean(axis=3).astype(k.dtype)
    gate = jnp.einsum('bhsd,bhnd->bhsn', q, k_mean).astype(jnp.float32)
    q_blk = jnp.arange(s_pad) // MOBA_BLOCK
    past = jnp.arange(n_blk)[None, :] < q_blk[:, None]
    gate = jnp.where(past, gate, -jnp.inf)
    n_sel = min(MOBA_TOPK, n_blk)
    _, sel_idx = lax.top_k(gate, n_sel)
    sel_valid = sel_idx < q_blk[None, None, :, None]
    scale = Dh ** -0.5
    b_ix = jnp.arange(B)[:, None, None, None]
    h_ix = jnp.arange(H)[None, :, None, None]

    def one_chunk(c):
        start = c * MOBA_QCHUNK
        qs = lax.dynamic_slice_in_dim(q, start, MOBA_QCHUNK, axis=2)
        idx = lax.dynamic_slice_in_dim(sel_idx, start, MOBA_QCHUNK, axis=2)
        valid = lax.dynamic_slice_in_dim(sel_valid, start, MOBA_QCHUNK, axis=2)
        blk_start = (start // MOBA_BLOCK) * MOBA_BLOCK
        k_own = lax.dynamic_slice_in_dim(k, blk_start, MOBA_BLOCK, axis=2)
        v_own = lax.dynamic_slice_in_dim(v, blk_start, MOBA_BLOCK, axis=2)
        q_pos = start + jnp.arange(MOBA_QCHUNK)
        k_pos = blk_start + jnp.arange(MOBA_BLOCK)
        s_own = jnp.einsum('bhqd,bhkd->bhqk', qs, k_own).astype(jnp.float32) * scale
        s_own = jnp.where(k_pos[None, :] <= q_pos[:, None], s_own, -jnp.inf)
        k_sel = k_blocks[b_ix, h_ix, idx]
        v_sel = v_blocks[b_ix, h_ix, idx]
        s_sel = jnp.einsum('bhqd,bhqjkd->bhqjk', qs, k_sel).astype(jnp.float32) * scale
        s_sel = jnp.where(valid[..., None], s_sel, -jnp.inf)
        s_all = jnp.concatenate(
            [s_own, s_sel.reshape(B, H, MOBA_QCHUNK, n_sel * MOBA_BLOCK)], axis=-1)
        p = jax.nn.softmax(s_all, axis=-1).astype(v.dtype)
        p_own = p[..., :MOBA_BLOCK]
        p_sel = p[..., MOBA_BLOCK:].reshape(B, H, MOBA_QCHUNK, n_sel, MOBA_BLOCK)
        return (jnp.einsum('bhqk,bhkd->bhqd', p_own, v_own)
                + jnp.einsum('bhqjk,bhqjkd->bhqd', p_sel, v_sel))

    o = lax.map(one_chunk, jnp.arange(s_pad // MOBA_QCHUNK))
    o = o.transpose(1, 0, 3, 2, 4).reshape(B, s_pad, H * Dh)[:, :S]
    return o @ w_out


def diff_attn_mixer(x, w_in, w_out, lam_q1, lam_k1, lam_q2, lam_k2, subln_g, cos, sin, layer_idx):
    B, S, _ = x.shape
    H, dh = B_HEADS, B_HEAD_DIM
    qkv = x @ w_in
    q = apply_rotary(qkv[..., :D_MODEL].reshape(B, S, 2 * H, dh), cos, sin)
    k = apply_rotary(qkv[..., D_MODEL:2 * D_MODEL].reshape(B, S, 2 * H, dh), cos, sin)
    v = qkv[..., 2 * D_MODEL:].reshape(B, S, H, 2 * dh).transpose(0, 2, 1, 3)
    q = q.transpose(0, 2, 1, 3).reshape(B, H, 2, S, dh)
    k = k.transpose(0, 2, 1, 3).reshape(B, H, 2, S, dh)
    lam_init = 0.8 - 0.6 * math.exp(-0.3 * layer_idx)
    lam = (jnp.exp(jnp.sum(lam_q1.astype(jnp.float32) * lam_k1.astype(jnp.float32)))
           - jnp.exp(jnp.sum(lam_q2.astype(jnp.float32) * lam_k2.astype(jnp.float32)))
           + lam_init)
    scale = dh ** -0.5
    k_pos = jnp.arange(S)

    def one_block(c):
        start = c * B_QBLOCK
        qs = lax.dynamic_slice_in_dim(q, start, B_QBLOCK, axis=3)
        s = jnp.einsum('bhiqd,bhikd->bhiqk', qs, k).astype(jnp.float32) * scale
        q_pos = start + jnp.arange(B_QBLOCK)
        s = jnp.where(k_pos[None, :] <= q_pos[:, None], s, -jnp.inf)
        p = jax.nn.softmax(s, axis=-1)
        a = p[:, :, 0] - lam * p[:, :, 1]
        return jnp.einsum('bhqk,bhkd->bhqd', a.astype(v.dtype), v)

    o = lax.map(one_block, jnp.arange(S // B_QBLOCK))
    o = o.transpose(1, 0, 3, 2, 4).reshape(B, S, H, 2 * dh)
    o = rmsnorm(o, subln_g, eps=SUBLN_EPS) * (1.0 - lam_init)
    return o.reshape(B, S, H * 2 * dh) @ w_out


def rglru_mixer(x, w_in, conv_w, conv_b, w_a, b_a, w_x, b_x, lam, w_out):
    B, S, _ = x.shape
    xy = x @ w_in
    gate_branch = jax.nn.gelu(xy[..., :C_WIDTH])
    rec = causal_dwconv(xy[..., C_WIDTH:], conv_w, conv_b)
    xb = rec.reshape(B, S, C_BLOCKS, C_BLOCK_DIM)
    r = jax.nn.sigmoid(jnp.einsum('bsgi,gij->bsgj', xb, w_a) + b_a).reshape(B, S, C_WIDTH)
    i = jax.nn.sigmoid(jnp.einsum('bsgi,gij->bsgj', xb, w_x) + b_x).reshape(B, S, C_WIDTH)
    log_a = -LRU_C * r.astype(jnp.float32) * jax.nn.softplus(-lam.astype(jnp.float32))
    a = jnp.exp(log_a)
    mult = jnp.sqrt(-jnp.expm1(2.0 * log_a))
    b = mult * (i * rec).astype(jnp.float32)

    def combine(left, right):
        a1, b1 = left
        a2, b2 = right
        return a1 * a2, a2 * b1 + b2

    _, h = lax.associative_scan(combine, (a, b), axis=1)
    y = h.astype(x.dtype) * gate_branch
    return y @ w_out


def retention_mixer(x, w_in, gn_g, w_out, cos, sin):
    B, S, _ = x.shape
    H, dk, dv = D_HEADS, D_QK_DIM, D_V_DIM
    proj = x @ w_in
    q = apply_rotary(proj[..., :H * dk].reshape(B, S, H, dk), cos, sin)
    k = apply_rotary(proj[..., H * dk:2 * H * dk].reshape(B, S, H, dk), cos, sin) * (dk ** -0.5)
    v = proj[..., 2 * H * dk:2 * H * dk + H * dv].reshape(B, S, H, dv)
    g = proj[..., 2 * H * dk + H * dv:]
    n_chunk = S // RET_CHUNK

    def to_chunks(t):
        return t.reshape(B, n_chunk, RET_CHUNK, H, t.shape[-1]).transpose(1, 0, 3, 2, 4).astype(jnp.float32)

    log_gamma = jnp.log1p(-jnp.exp2(-5.0 - jnp.arange(H, dtype=jnp.float32)))
    pos = jnp.arange(RET_CHUNK, dtype=jnp.float32)
    diff = pos[:, None] - pos[None, :]
    decay_in = jnp.where(diff >= 0, jnp.exp(jnp.maximum(diff, 0.0)[None] * log_gamma[:, None, None]), 0.0)
    xi = jnp.exp((pos + 1.0)[None, :] * log_gamma[:, None])
    zeta = jnp.exp((RET_CHUNK - 1.0 - pos)[None, :] * log_gamma[:, None])
    gamma_chunk = jnp.exp(RET_CHUNK * log_gamma)

    def step(state, qkv_c):
        qc, kc, vc = qkv_c
        inner = jnp.einsum('bhqd,bhkd->bhqk', qc, kc) * decay_in
        o = (jnp.einsum('bhqk,bhkv->bhqv', inner, vc)
             + jnp.einsum('bhqd,bhdv->bhqv', qc, state) * xi[None, :, :, None])
        state = (gamma_chunk[None, :, None, None] * state
                 + jnp.einsum('bhkd,bhkv->bhdv', kc * zeta[None, :, :, None], vc))
        return state, o

    state0 = jnp.zeros((B, H, dk, dv), jnp.float32)
    _, o = lax.scan(step, state0, (to_chunks(q), to_chunks(k), to_chunks(v)))
    o = o.transpose(1, 0, 3, 2, 4).reshape(B, S, H, dv)
    mu = jnp.mean(o, axis=-1, keepdims=True)
    var = jnp.mean(jnp.square(o - mu), axis=-1, keepdims=True)
    o = (o - mu) * lax.rsqrt(var + GN_EPS) * gn_g.astype(jnp.float32).reshape(H, dv)
    y = jax.nn.silu(g) * o.reshape(B, S, H * dv).astype(x.dtype)
    return y @ w_out


def conv_ffn(x, w_in, conv_w, conv_b, w_out):
    h = x @ w_in
    gate = causal_dwconv(h[..., :D_FF], conv_w, conv_b)
    return (jax.nn.silu(gate) * h[..., D_FF:]) @ w_out


def setup_inputs(seed: int = 0) -> dict:
    key = jax.random.key(seed)
    ks = iter(jax.random.split(key, 40))
    f32 = jnp.float32

    def normal(shape, scale):
        return jax.random.normal(next(ks), shape, f32) * scale

    def gain(shape):
        return 1.0 + normal(shape, 0.02)

    n_a, n_b, n_c, n_d = (len(range(m, DEPTH, N_MIXERS)) for m in range(N_MIXERS))
    D = D_MODEL
    x = normal((BATCH, SEQ, D), 1.0)
    norm_mix_g = gain((DEPTH, D))
    norm_ffn_g = gain((DEPTH, D))
    norm_final_g = gain((D,))
    a_w_in = normal((n_a, D, 3 * A_HEADS * A_HEAD_DIM), D ** -0.5)
    a_w_out = normal((n_a, A_HEADS * A_HEAD_DIM, D), D ** -0.5)
    b_w_in = normal((n_b, D, 3 * D), D ** -0.5)
    b_w_out = normal((n_b, D, D), D ** -0.5)
    b_lam_q1 = normal((n_b, B_HEAD_DIM), 0.1)
    b_lam_k1 = normal((n_b, B_HEAD_DIM), 0.1)
    b_lam_q2 = normal((n_b, B_HEAD_DIM), 0.1)
    b_lam_k2 = normal((n_b, B_HEAD_DIM), 0.1)
    b_subln_g = gain((n_b, 2 * B_HEAD_DIM))
    c_w_in = normal((n_c, D, 2 * C_WIDTH), D ** -0.5)
    c_conv_w = normal((n_c, C_CONV, C_WIDTH), C_CONV ** -0.5)
    c_conv_b = normal((n_c, C_WIDTH), 0.01)
    c_w_a = normal((n_c, C_BLOCKS, C_BLOCK_DIM, C_BLOCK_DIM), C_BLOCK_DIM ** -0.5)
    c_b_a = normal((n_c, C_BLOCKS, C_BLOCK_DIM), 0.01)
    c_w_x = normal((n_c, C_BLOCKS, C_BLOCK_DIM, C_BLOCK_DIM), C_BLOCK_DIM ** -0.5)
    c_b_x = normal((n_c, C_BLOCKS, C_BLOCK_DIM), 0.01)
    u = jax.random.uniform(next(ks), (n_c, C_WIDTH), f32, 0.9, 0.999)
    a0 = u ** (1.0 / LRU_C)
    c_lambda = jnp.log(a0) - jnp.log1p(-a0)
    c_w_out = normal((n_c, C_WIDTH, D), C_WIDTH ** -0.5)
    d_w_in = normal((n_d, D, 2 * D_HEADS * D_QK_DIM + 2 * D_HEADS * D_V_DIM), D ** -0.5)
    d_gn_g = gain((n_d, D_HEADS * D_V_DIM))
    d_w_out = normal((n_d, D_HEADS * D_V_DIM, D), (D_HEADS * D_V_DIM) ** -0.5)
    ffn_w_in = normal((DEPTH, D, 2 * D_FF), D ** -0.5)
    ffn_conv_w = normal((DEPTH, FFN_CONV, D_FF), FFN_CONV ** -0.5)
    ffn_conv_b = normal((DEPTH, D_FF), 0.01)
    ffn_w_out = normal((DEPTH, D_FF, D), D_FF ** -0.5)
    return {
        "x": x, "norm_mix_g": norm_mix_g, "norm_ffn_g": norm_ffn_g, "norm_final_g": norm_final_g,
        "a_w_in": a_w_in, "a_w_out": a_w_out,
        "b_w_in": b_w_in, "b_w_out": b_w_out, "b_lam_q1": b_lam_q1, "b_lam_k1": b_lam_k1,
        "b_lam_q2": b_lam_q2, "b_lam_k2": b_lam_k2, "b_subln_g": b_subln_g,
        "c_w_in": c_w_in, "c_conv_w": c_conv_w, "c_conv_b": c_conv_b, "c_w_a": c_w_a, "c_b_a": c_b_a,
        "c_w_x": c_w_x, "c_b_x": c_b_x, "c_lambda": c_lambda, "c_w_out": c_w_out,
        "d_w_in": d_w_in, "d_gn_g": d_gn_g, "d_w_out": d_w_out,
        "ffn_w_in": ffn_w_in, "ffn_conv_w": ffn_conv_w, "ffn_conv_b": ffn_conv_b, "ffn_w_out": ffn_w_out,
    }


def reference(x, norm_mix_g, norm_ffn_g, norm_final_g,
              a_w_in, a_w_out,
              b_w_in, b_w_out, b_lam_q1, b_lam_k1, b_lam_q2, b_lam_k2, b_subln_g,
              c_w_in, c_conv_w, c_conv_b, c_w_a, c_b_a, c_w_x, c_b_x, c_lambda, c_w_out,
              d_w_in, d_gn_g, d_w_out,
              ffn_w_in, ffn_conv_w, ffn_conv_b, ffn_w_out):
    S = x.shape[1]
    cos_a, sin_a = rope_tables(S, A_HEAD_DIM // ROPE_FRACTION, ROPE_THETA)
    cos_b, sin_b = rope_tables(S, B_HEAD_DIM // ROPE_FRACTION, ROPE_THETA)
    cos_d, sin_d = rope_tables(S, D_QK_DIM, RET_THETA)
    h = x
    for i in range(DEPTH):
        m, j = i % N_MIXERS, i // N_MIXERS
        hn = rmsnorm(h, norm_mix_g[i])
        if m == 0:
            mix = moba_mixer(hn, a_w_in[j], a_w_out[j], cos_a, sin_a)
        elif m == 1:
            mix = diff_attn_mixer(hn, b_w_in[j], b_w_out[j], b_lam_q1[j], b_lam_k1[j],
                                  b_lam_q2[j], b_lam_k2[j], b_subln_g[j], cos_b, sin_b, i)
        elif m == 2:
            mix = rglru_mixer(hn, c_w_in[j], c_conv_w[j], c_conv_b[j], c_w_a[j], c_b_a[j],
                              c_w_x[j], c_b_x[j], c_lambda[j], c_w_out[j])
        else:
            mix = retention_mixer(hn, d_w_in[j], d_gn_g[j], d_w_out[j], cos_d, sin_d)
        h = h + mix
        h = h + conv_ffn(rmsnorm(h, norm_ffn_g[i]), ffn_w_in[i], ffn_conv_w[i], ffn_conv_b[i], ffn_w_out[i])
    return rmsnorm(h, norm_final_g)
```

```python
import functools
import math

import jax
import jax.numpy as jnp
from jax import lax
from jax.experimental import pallas as pl
from jax.experimental.pallas import tpu as pltpu

F32 = jnp.float32
BF16 = jnp.bfloat16

RMS_EPS = 1e-6
ROPE_THETA = 500000.0
A_HEADS = 8
MOBA_BLOCK = 256
MOBA_TOPK = 3
B_HEADS = 8
SUBLN_EPS = 1e-5
C_BLOCK_DIM = 128
LRU_C = 8.0
D_HEADS = 4
RET_THETA = 10000.0
GN_EPS = 1e-5
FFN_CONV = 3

VMEM_LIMIT_BYTES = 56 * 1024 * 1024
LANES = 128
ROT_CHUNK = 256
FFN_CHUNK = 256
BF16_ROWS = 16


def _cparams(*sem):
    return pltpu.CompilerParams(dimension_semantics=sem, vmem_limit_bytes=VMEM_LIMIT_BYTES)


def _resident(shape, index_map):
    return pl.BlockSpec(shape, index_map, pipeline_mode=pl.Buffered(1))


def _dot(a, b):
    return jnp.dot(a, b, preferred_element_type=F32)


def _dot_nt(a, b):
    return lax.dot_general(a, b, (((1,), (1,)), ((), ())), preferred_element_type=F32)


def _dot_tn(a, b):
    return lax.dot_general(a, b, (((0,), (0,)), ((), ())), preferred_element_type=F32)


def _rms(x, gain, eps):
    ms = jnp.mean(x * x, axis=-1, keepdims=True)
    return x * lax.rsqrt(ms + eps) * gain


def _norm_proj_kernel(*refs, n_rot, half):
    if n_rot:
        x_ref, g_ref, w_ref, c_ref, s1_ref, s2_ref, o_ref = refs
    else:
        x_ref, g_ref, w_ref, o_ref = refs
    xn = _rms(x_ref[...], g_ref[...], RMS_EPS).astype(BF16)
    n_out = o_ref.shape[1]
    for c in range(n_out // ROT_CHUNK):
        cols = slice(c * ROT_CHUNK, (c + 1) * ROT_CHUNK)
        y = _dot(xn, w_ref[:, cols])
        if c * ROT_CHUNK < n_rot:
            y = (y * c_ref[...]
                 + pltpu.roll(y, ROT_CHUNK - half, 1) * s1_ref[...]
                 + pltpu.roll(y, half, 1) * s2_ref[...])
        o_ref[:, cols] = y.astype(BF16)


def _rope_tables(seq, period, rot_dim, theta):
    half = rot_dim // 2
    inv = theta ** (-jnp.arange(0, rot_dim, 2, dtype=F32) / rot_dim)
    ang = jnp.arange(seq, dtype=F32)[:, None] * inv[None, :]
    cos, sin = jnp.cos(ang), jnp.sin(ang)
    rest = period - rot_dim
    one, zero_h, zero_r = jnp.ones((seq, rest), F32), jnp.zeros((seq, half), F32), jnp.zeros((seq, rest), F32)
    reps = ROT_CHUNK // period
    c = jnp.tile(jnp.concatenate([cos, cos, one], -1), (1, reps))
    s1 = jnp.tile(jnp.concatenate([-sin, zero_h, zero_r], -1), (1, reps))
    s2 = jnp.tile(jnp.concatenate([zero_h, sin, zero_r], -1), (1, reps))
    return c, s1, s2


def _norm_proj(h, gain, w, seq, rope=None, n_rot=0, half=0, tm=512):
    n, d = h.shape
    n_out = w.shape[1]
    nt = seq // tm
    in_specs = [pl.BlockSpec((tm, d), lambda i: (i, 0)),
                _resident((1, d), lambda i: (0, 0)),
                _resident((d, n_out), lambda i: (0, 0))]
    args = [h, gain.reshape(1, d), w]
    if n_rot:
        in_specs += [pl.BlockSpec((tm, ROT_CHUNK), lambda i: (i % nt, 0))] * 3
        args += list(rope)
    return pl.pallas_call(
        functools.partial(_norm_proj_kernel, n_rot=n_rot, half=half),
        grid=(n // tm,),
        in_specs=in_specs,
        out_specs=pl.BlockSpec((tm, n_out), lambda i: (i, 0)),
        out_shape=jax.ShapeDtypeStruct((n, n_out), BF16),
        compiler_params=_cparams("parallel"),
        name="norm_proj",
    )(*args)


def _out_proj_kernel(y_ref, w_ref, h_ref, o_ref):
    o_ref[...] = h_ref[...] + _dot(y_ref[...], w_ref[...])


def _out_proj(y, w, h, tm=512):
    n, k = y.shape
    d = w.shape[1]
    return pl.pallas_call(
        _out_proj_kernel,
        grid=(n // tm,),
        in_specs=[pl.BlockSpec((tm, k), lambda i: (i, 0)),
                  _resident((k, d), lambda i: (0, 0)),
                  pl.BlockSpec((tm, d), lambda i: (i, 0))],
        out_specs=pl.BlockSpec((tm, d), lambda i: (i, 0)),
        out_shape=jax.ShapeDtypeStruct((n, d), F32),
        compiler_params=_cparams("parallel"),
        name="out_proj",
    )(y, w, h)


def _ffn_kernel(h_ref, halo_ref, g_ref, wg_ref, wu_ref, cw_ref, cb_ref, wo_ref, gf_ref, o_ref,
                xn_ref, acc_ref, *, nt, final_norm):
    i = pl.program_id(0)
    tm = h_ref.shape[0]
    gain = g_ref[...]
    xn_ref[0:BF16_ROWS, :] = _rms(halo_ref[...], gain, RMS_EPS).astype(BF16)
    xn_ref[BF16_ROWS:, :] = _rms(h_ref[...], gain, RMS_EPS).astype(BF16)
    n_zero = jnp.where(i % nt == 0, BF16_ROWS, 0)
    keep = lax.broadcasted_iota(jnp.int32, (tm + BF16_ROWS, 1), 0) >= n_zero
    acc_ref[...] = jnp.zeros_like(acc_ref)

    def body(j, carry):
        hg = _dot(xn_ref[...], wg_ref[j])
        hg = jnp.where(keep, hg, 0.0)
        cw = cw_ref[j]
        gt = (cw[2:3] * hg + cw[1:2] * pltpu.roll(hg, 1, 0) + cw[0:1] * pltpu.roll(hg, 2, 0))
        gt = gt[BF16_ROWS:] + cb_ref[j]
        up = _dot(xn_ref[BF16_ROWS:, :], wu_ref[j])
        z = (gt * jax.nn.sigmoid(gt) * up).astype(BF16)
        acc_ref[...] += _dot(z, wo_ref[j])
        return carry

    lax.fori_loop(0, wg_ref.shape[0], body, 0)
    out = h_ref[...] + acc_ref[...]
    if final_norm:
        out = _rms(out, gf_ref[...], RMS_EPS)
    o_ref[...] = out


def _ffn(h, gain, w_in, conv_w, conv_b, w_out, final_gain, seq, final_norm, tm=512):
    n, d = h.shape
    d_ff = w_out.shape[0]
    nj = d_ff // FFN_CHUNK
    wg = w_in[:, :d_ff].reshape(d, nj, FFN_CHUNK).transpose(1, 0, 2).astype(BF16)
    wu = w_in[:, d_ff:].reshape(d, nj, FFN_CHUNK).transpose(1, 0, 2).astype(BF16)
    wo = w_out.reshape(nj, FFN_CHUNK, d).astype(BF16)
    cw = conv_w.reshape(FFN_CONV, nj, FFN_CHUNK).transpose(1, 0, 2)
    cb = conv_b.reshape(nj, 1, FFN_CHUNK)
    halo_blocks = tm // BF16_ROWS
    return pl.pallas_call(
        functools.partial(_ffn_kernel, nt=seq // tm, final_norm=final_norm),
        grid=(n // tm,),
        in_specs=[pl.BlockSpec((tm, d), lambda i: (i, 0)),
                  pl.BlockSpec((BF16_ROWS, d), lambda i: (jnp.maximum(i * halo_blocks - 1, 0), 0)),
                  _resident((1, d), lambda i: (0, 0)),
                  _resident((nj, d, FFN_CHUNK), lambda i: (0, 0, 0)),
                  _resident((nj, d, FFN_CHUNK), lambda i: (0, 0, 0)),
                  _resident((nj, FFN_CONV, FFN_CHUNK), lambda i: (0, 0, 0)),
                  _resident((nj, 1, FFN_CHUNK), lambda i: (0, 0, 0)),
                  _resident((nj, FFN_CHUNK, d), lambda i: (0, 0, 0)),
                  _resident((1, d), lambda i: (0, 0))],
        out_specs=pl.BlockSpec((tm, d), lambda i: (i, 0)),
        out_shape=jax.ShapeDtypeStruct((n, d), F32),
        scratch_shapes=[pltpu.VMEM((tm + BF16_ROWS, d), BF16), pltpu.VMEM((tm, d), F32)],
        compiler_params=_cparams("parallel"),
        name="conv_ffn",
    )(h, h, gain.reshape(1, d), wg, wu, cw, cb, wo, final_gain.reshape(1, d))


def _moba_kernel(q_ref, k_ref, v_ref, o_ref, kmean_ref, sel_ref, m_ref, l_ref, acc_ref, *, nb, scale):
    qi = pl.program_id(2)
    blk, dh = q_ref.shape

    @pl.when(qi == 0)
    def _():
        kf = k_ref[...].astype(F32).reshape(nb, blk, dh)
        kmean_ref[...] = (jnp.sum(kf, axis=1) * (1.0 / blk)).astype(BF16)

    q = q_ref[...]
    gate = _dot_nt(kmean_ref[...], q)
    nidx = lax.broadcasted_iota(jnp.int32, (nb, blk), 0)
    past = nidx < qi
    gate = jnp.where(past, gate, -jnp.inf)
    rank = jnp.zeros((nb, blk), jnp.int32)
    for m in range(nb):
        gm = gate[m:m + 1, :]
        rank = rank + jnp.where(gm == gate, jnp.where(m < nidx, 1, 0), jnp.where(gm > gate, 1, 0))
    sel_t = jnp.where(past, (rank < MOBA_TOPK).astype(F32), 0.0)
    sel_t = jnp.concatenate([sel_t, jnp.zeros((LANES - nb, blk), F32)], axis=0)
    sel_ref[...] = sel_t.T

    row0 = pl.multiple_of(qi * blk, blk)
    s = _dot_nt(q, k_ref[pl.ds(row0, blk), :]) * scale
    r_id = lax.broadcasted_iota(jnp.int32, (blk, blk), 0)
    c_id = lax.broadcasted_iota(jnp.int32, (blk, blk), 1)
    s = jnp.where(c_id <= r_id, s, -jnp.inf)
    m0 = jnp.max(s, axis=-1, keepdims=True)
    p = jnp.exp(s - m0)
    m_ref[...] = m0
    l_ref[...] = jnp.sum(p, axis=-1, keepdims=True)
    acc_ref[...] = _dot(p.astype(BF16), v_ref[pl.ds(row0, blk), :])

    lane = lax.broadcasted_iota(jnp.int32, (blk, LANES), 1)

    def body(n, carry):
        chosen = jnp.sum(jnp.where(lane == n, sel_ref[...], 0.0), axis=-1, keepdims=True) > 0.5
        rows = pl.ds(pl.multiple_of(n * blk, blk), blk)
        s = _dot_nt(q, k_ref[rows, :]) * scale
        s = jnp.where(chosen, s, -jnp.inf)
        m_old = m_ref[...]
        m_new = jnp.maximum(m_old, jnp.max(s, axis=-1, keepdims=True))
        alpha = jnp.exp(m_old - m_new)
        p = jnp.exp(s - m_new)
        l_ref[...] = alpha * l_ref[...] + jnp.sum(p, axis=-1, keepdims=True)
        acc_ref[...] = alpha * acc_ref[...] + _dot(p.astype(BF16), v_ref[rows, :])
        m_ref[...] = m_new
        return carry

    lax.fori_loop(0, qi, body, 0)
    o_ref[...] = (acc_ref[...] / l_ref[...]).astype(BF16)


def _moba_attention(qkv, batch, seq, heads, dh):
    assert seq % MOBA_BLOCK == 0
    nb = seq // MOBA_BLOCK
    assert nb <= LANES and nb % 8 == 0
    blk = MOBA_BLOCK
    return pl.pallas_call(
        functools.partial(_moba_kernel, nb=nb, scale=dh ** -0.5),
        grid=(batch, heads, nb),
        in_specs=[pl.BlockSpec((blk, dh), lambda b, h, i: (b * nb + i, h)),
                  pl.BlockSpec((seq, dh), lambda b, h, i: (b, heads + h)),
                  pl.BlockSpec((seq, dh), lambda b, h, i: (b, 2 * heads + h))],
        out_specs=pl.BlockSpec((blk, dh), lambda b, h, i: (b * nb + i, h)),
        out_shape=jax.ShapeDtypeStruct((batch * seq, heads * dh), BF16),
        scratch_shapes=[pltpu.VMEM((nb, dh), BF16), pltpu.VMEM((blk, LANES), F32),
                        pltpu.VMEM((blk, 1), F32), pltpu.VMEM((blk, 1), F32), pltpu.VMEM((blk, dh), F32)],
        compiler_params=_cparams("parallel", "parallel", "arbitrary"),
        name="moba_attention",
    )(qkv, qkv, qkv)


def _diff_attn_kernel(q_ref, k_ref, v_ref, lam_ref, g_ref, o_ref, m_ref, l_ref, acc_ref, *, scale, lam_init):
    qi = pl.program_id(2)
    tq, width = q_ref.shape
    q = q_ref[...] * scale
    lane = lax.broadcasted_iota(jnp.int32, (tq, width), 1)
    zero = jnp.zeros_like(q)
    qq = jnp.concatenate([jnp.where(lane < width // 2, q, zero), jnp.where(lane >= width // 2, q, zero)], axis=0)

    row0 = pl.multiple_of(qi * tq, tq)
    s = _dot_nt(qq, k_ref[pl.ds(row0, tq), :])
    r_id = lax.broadcasted_iota(jnp.int32, (2 * tq, tq), 0)
    r_id = jnp.where(r_id >= tq, r_id - tq, r_id)
    c_id = lax.broadcasted_iota(jnp.int32, (2 * tq, tq), 1)
    s = jnp.where(c_id <= r_id, s, -jnp.inf)
    m0 = jnp.max(s, axis=-1, keepdims=True)
    p = jnp.exp(s - m0)
    m_ref[...] = m0
    l_ref[...] = jnp.sum(p, axis=-1, keepdims=True)
    acc_ref[...] = _dot(p.astype(BF16), v_ref[pl.ds(row0, tq), :])

    def body(n, carry):
        rows = pl.ds(pl.multiple_of(n * tq, tq), tq)
        s = _dot_nt(qq, k_ref[rows, :])
        m_old = m_ref[...]
        m_new = jnp.maximum(m_old, jnp.max(s, axis=-1, keepdims=True))
        alpha = jnp.exp(m_old - m_new)
        p = jnp.exp(s - m_new)
        l_ref[...] = alpha * l_ref[...] + jnp.sum(p, axis=-1, keepdims=True)
        acc_ref[...] = alpha * acc_ref[...] + _dot(p.astype(BF16), v_ref[rows, :])
        m_ref[...] = m_new
        return carry

    lax.fori_loop(0, qi, body, 0)

    lv = lam_ref[...]
    lam = (jnp.exp(jnp.sum(lv[0:1] * lv[1:2], axis=-1, keepdims=True))
           - jnp.exp(jnp.sum(lv[2:3] * lv[3:4], axis=-1, keepdims=True)) + lam_init)
    o = acc_ref[...] / l_ref[...]
    o = o[:tq] - lam * o[tq:]
    o_ref[...] = (_rms(o, g_ref[...], SUBLN_EPS) * (1.0 - lam_init)).astype(BF16)


def _diff_attention(qkv, lam_vecs, subln_g, batch, seq, heads, lam_init, tq=256):
    width = qkv.shape[1] // (3 * heads)
    nq = seq // tq
    dh = width // 2
    return pl.pallas_call(
        functools.partial(_diff_attn_kernel, scale=dh ** -0.5, lam_init=lam_init),
        grid=(batch, heads, nq),
        in_specs=[pl.BlockSpec((tq, width), lambda b, h, i: (b * nq + i, h)),
                  pl.BlockSpec((seq, width), lambda b, h, i: (b, heads + h)),
                  pl.BlockSpec((seq, width), lambda b, h, i: (b, 2 * heads + h)),
                  _resident(lam_vecs.shape, lambda b, h, i: (0, 0)),
                  _resident((1, width), lambda b, h, i: (0, 0))],
        out_specs=pl.BlockSpec((tq, width), lambda b, h, i: (b * nq + i, h)),
        out_shape=jax.ShapeDtypeStruct((batch * seq, heads * width), BF16),
        scratch_shapes=[pltpu.VMEM((2 * tq, 1), F32), pltpu.VMEM((2 * tq, 1), F32),
                        pltpu.VMEM((2 * tq, width), F32)],
        compiler_params=_cparams("parallel", "parallel", "arbitrary"),
        name="diff_attention",
    )(qkv, qkv, qkv, lam_vecs, subln_g.reshape(1, width))


def _rglru_kernel(gate_ref, x_ref, cw_ref, cb_ref, wa_ref, ba_ref, wx_ref, bx_ref, lam_ref, o_ref,
                  prev_ref, hprev_ref, *, conv_width):
    t = pl.program_id(2)
    ts, tc = x_ref.shape
    halo = prev_ref.shape[0]

    @pl.when(t == 0)
    def _():
        prev_ref[...] = jnp.zeros_like(prev_ref)
        hprev_ref[...] = jnp.zeros_like(hprev_ref)

    x = x_ref[...].astype(F32)
    full = jnp.concatenate([prev_ref[...], x], axis=0)
    prev_ref[...] = x[ts - halo:, :]
    cw = cw_ref[...]
    rec = cw[conv_width - 1:conv_width] * full
    for k in range(1, conv_width):
        rec = rec + cw[conv_width - 1 - k:conv_width - k] * pltpu.roll(full, k, 0)
    rec = rec[halo:] + cb_ref[...]

    rec_bf = rec.astype(BF16)
    r_parts, i_parts = [], []
    for g in range(tc // C_BLOCK_DIM):
        cols = slice(g * C_BLOCK_DIM, (g + 1) * C_BLOCK_DIM)
        r_parts.append(_dot(rec_bf[:, cols], wa_ref[g]))
        i_parts.append(_dot(rec_bf[:, cols], wx_ref[g]))
    r = jax.nn.sigmoid(jnp.concatenate(r_parts, axis=-1) + ba_ref[...])
    ig = jax.nn.sigmoid(jnp.concatenate(i_parts, axis=-1) + bx_ref[...])

    softplus = jnp.log1p(jnp.exp(-lam_ref[...]))
    log_a = (-LRU_C) * r * softplus
    a = jnp.exp(log_a)
    b = jnp.sqrt(1.0 - a * a) * (ig * rec)

    row = lax.broadcasted_iota(jnp.int32, (ts, 1), 0)
    d = 1
    while d < ts:
        ok = row >= d
        b = jnp.where(ok, a * pltpu.roll(b, d, 0) + b, b)
        a = jnp.where(ok, a * pltpu.roll(a, d, 0), a)
        d *= 2
    h = a * hprev_ref[...] + b
    hprev_ref[...] = h[ts - 1:ts, :]
    o_ref[...] = (h * jax.nn.gelu(gate_ref[...].astype(F32))).astype(BF16)


def _rglru(xy, conv_w, conv_b, w_a, b_a, w_x, b_x, lam, batch, seq, ts=256, tc=256):
    width = xy.shape[1] // 2
    nt, ncb = seq // ts, width // tc
    gpb = tc // C_BLOCK_DIM
    conv_width = conv_w.shape[0]
    vec = lambda v: v.reshape(1, width)
    vspec = pl.BlockSpec((1, tc), lambda b, c, t: (0, c))
    wspec = pl.BlockSpec((gpb, C_BLOCK_DIM, C_BLOCK_DIM), lambda b, c, t: (c, 0, 0))
    return pl.pallas_call(
        functools.partial(_rglru_kernel, conv_width=conv_width),
        grid=(batch, ncb, nt),
        in_specs=[pl.BlockSpec((ts, tc), lambda b, c, t: (b * nt + t, c)),
                  pl.BlockSpec((ts, tc), lambda b, c, t: (b * nt + t, ncb + c)),
                  pl.BlockSpec((conv_width, tc), lambda b, c, t: (0, c)),
                  vspec, wspec, vspec, wspec, vspec, vspec],
        out_specs=pl.BlockSpec((ts, tc), lambda b, c, t: (b * nt + t, c)),
        out_shape=jax.ShapeDtypeStruct((batch * seq, width), BF16),
        scratch_shapes=[pltpu.VMEM((8, tc), F32), pltpu.VMEM((1, tc), F32)],
        compiler_params=_cparams("parallel", "parallel", "arbitrary"),
        name="rglru",
    )(xy, xy, conv_w, vec(conv_b), w_a.astype(BF16), vec(b_a), w_x.astype(BF16), vec(b_x), vec(lam))


def _retention_kernel(q_ref, k_ref, v_ref, g_ref, gn_ref, o_ref, state_ref, decay_ref, *, k_scale):
    hd = pl.program_id(1)
    c = pl.program_id(2)
    ch = q_ref.shape[0]

    log_gamma = jnp.log1p(-jnp.exp2(-5.0 - jnp.full((1, 1), hd, jnp.int32).astype(F32)))

    @pl.when(c == 0)
    def _():
        state_ref[...] = jnp.zeros_like(state_ref)
        r_id = lax.broadcasted_iota(jnp.int32, (ch, ch), 0)
        c_id = lax.broadcasted_iota(jnp.int32, (ch, ch), 1)
        diff = (r_id - c_id).astype(F32)
        decay_ref[...] = jnp.where(diff >= 0, jnp.exp(jnp.maximum(diff, 0.0) * log_gamma), 0.0)

    pos = lax.broadcasted_iota(jnp.int32, (ch, 1), 0).astype(F32)
    xi = jnp.exp((pos + 1.0) * log_gamma)
    zeta = jnp.exp((ch - 1.0 - pos) * log_gamma)
    gamma_chunk = jnp.exp(ch * log_gamma)

    q = q_ref[...]
    k = k_ref[...].astype(F32) * k_scale
    v = v_ref[...]
    inner = _dot_nt(q, k.astype(BF16)) * decay_ref[...]
    state = state_ref[...]
    o = _dot(inner.astype(BF16), v) + _dot(q, state.astype(BF16)) * xi
    state_ref[...] = gamma_chunk * state + _dot_tn((k * zeta).astype(BF16), v)

    mu = jnp.mean(o, axis=-1, keepdims=True)
    var = jnp.mean(jnp.square(o - mu), axis=-1, keepdims=True)
    o = (o - mu) * lax.rsqrt(var + GN_EPS) * gn_ref[...]
    g = g_ref[...].astype(F32)
    o_ref[...] = (g * jax.nn.sigmoid(g) * o).astype(BF16)


def _retention(proj, gn_g, batch, seq, heads, dk, dv, chunk=256):
    nc = seq // chunk
    kb, vb, gb = heads, (2 * heads * dk) // dv, (2 * heads * dk) // dv + heads
    return pl.pallas_call(
        functools.partial(_retention_kernel, k_scale=dk ** -0.5),
        grid=(batch, heads, nc),
        in_specs=[pl.BlockSpec((chunk, dk), lambda b, h, c: (b * nc + c, h)),
                  pl.BlockSpec((chunk, dk), lambda b, h, c: (b * nc + c, kb + h)),
                  pl.BlockSpec((chunk, dv), lambda b, h, c: (b * nc + c, vb + h)),
                  pl.BlockSpec((chunk, dv), lambda b, h, c: (b * nc + c, gb + h)),
                  pl.BlockSpec((1, dv), lambda b, h, c: (0, h))],
        out_specs=pl.BlockSpec((chunk, dv), lambda b, h, c: (b * nc + c, h)),
        out_shape=jax.ShapeDtypeStruct((batch * seq, heads * dv), BF16),
        scratch_shapes=[pltpu.VMEM((dk, dv), F32), pltpu.VMEM((chunk, chunk), F32)],
        compiler_params=_cparams("parallel", "parallel", "arbitrary"),
        name="retention",
    )(proj, proj, proj, proj, gn_g.reshape(1, heads * dv))


def kernel(x, norm_mix_g, norm_ffn_g, norm_final_g, a_w_in, a_w_out, b_w_in, b_w_out, b_lam_q1, b_lam_k1,
           b_lam_q2, b_lam_k2, b_subln_g, c_w_in, c_conv_w, c_conv_b, c_w_a, c_b_a, c_w_x, c_b_x, c_lambda,
           c_w_out, d_w_in, d_gn_g, d_w_out, ffn_w_in, ffn_conv_w, ffn_conv_b, ffn_w_out):
    batch, seq, d = x.shape
    depth = norm_mix_g.shape[0]
    n_mixers = 4
    h = x.reshape(batch * seq, d)
    for i in range(depth):
        m, j = i % n_mixers, i // n_mixers
        if m == 0:
            dh = d // A_HEADS
            rope = _rope_tables(seq, dh, dh // 4, ROPE_THETA)
            qkv = _norm_proj(h, norm_mix_g[i], a_w_in[j].astype(BF16), seq, rope, n_rot=2 * d, half=dh // 8)
            y = _moba_attention(qkv, batch, seq, A_HEADS, dh)
            h = _out_proj(y, a_w_out[j].astype(BF16), h)
        elif m == 1:
            dh = d // (2 * B_HEADS)
            rope = _rope_tables(seq, dh, dh // 4, ROPE_THETA)
            qkv = _norm_proj(h, norm_mix_g[i], b_w_in[j].astype(BF16), seq, rope, n_rot=2 * d, half=dh // 8)
            lam_vecs = jnp.stack([b_lam_q1[j], b_lam_k1[j], b_lam_q2[j], b_lam_k2[j]])
            lam_init = 0.8 - 0.6 * math.exp(-0.3 * i)
            y = _diff_attention(qkv, lam_vecs, b_subln_g[j], batch, seq, B_HEADS, lam_init)
            h = _out_proj(y, b_w_out[j].astype(BF16), h)
        elif m == 2:
            xy = _norm_proj(h, norm_mix_g[i], c_w_in[j].astype(BF16), seq)
            y = _rglru(xy, c_conv_w[j], c_conv_b[j], c_w_a[j], c_b_a[j], c_w_x[j], c_b_x[j], c_lambda[j],
                       batch, seq)
            h = _out_proj(y, c_w_out[j].astype(BF16), h)
        else:
            dk = d // D_HEADS
            rope = _rope_tables(seq, dk, dk, RET_THETA)
            proj = _norm_proj(h, norm_mix_g[i], d_w_in[j].astype(BF16), seq, rope, n_rot=2 * D_HEADS * dk,
                              half=dk // 2)
            y = _retention(proj, d_gn_g[j], batch, seq, D_HEADS, dk, 2 * dk)
            h = _out_proj(y, d_w_out[j].astype(BF16), h)
        h = _ffn(h, norm_ffn_g[i], ffn_w_in[i], ffn_conv_w[i], ffn_conv_b[i], ffn_w_out[i], norm_final_g, seq,
                 final_norm=(i == depth - 1))
    return h.reshape(batch, seq, d)
```

```python
import functools
import math

import jax
import jax.numpy as jnp
from jax import lax
from jax.experimental import pallas as pl
from jax.experimental.pallas import tpu as pltpu

F32 = jnp.float32
BF16 = jnp.bfloat16

RMS_EPS = 1e-6
ROPE_THETA = 500000.0
A_HEADS = 8
MOBA_BLOCK = 256
MOBA_TOPK = 3
B_HEADS = 8
SUBLN_EPS = 1e-5
C_BLOCK_DIM = 128
LRU_C = 8.0
D_HEADS = 4
RET_THETA = 10000.0
GN_EPS = 1e-5
FFN_CONV = 3

VMEM_LIMIT_BYTES = 56 * 1024 * 1024
LANES = 128
ROT_CHUNK = 256
FFN_CHUNK = 256
BF16_ROWS = 16


def _cparams(*sem):
    return pltpu.CompilerParams(dimension_semantics=sem, vmem_limit_bytes=VMEM_LIMIT_BYTES)


def _resident(shape, index_map):
    return pl.BlockSpec(shape, index_map, pipeline_mode=pl.Buffered(1))


def _dot(a, b):
    return jnp.dot(a, b, preferred_element_type=F32)


def _dot_nt(a, b):
    return lax.dot_general(a, b, (((1,), (1,)), ((), ())), preferred_element_type=F32)


def _dot_tn(a, b):
    return lax.dot_general(a, b, (((0,), (0,)), ((), ())), preferred_element_type=F32)


def _rms(x, gain, eps):
    ms = jnp.mean(x * x, axis=-1, keepdims=True)
    return x * lax.rsqrt(ms + eps) * gain


def _norm_proj_kernel(*refs, n_rot, half):
    if n_rot:
        x_ref, g_ref, w_ref, c_ref, s1_ref, s2_ref, o_ref = refs
    else:
        x_ref, g_ref, w_ref, o_ref = refs
    xn = _rms(x_ref[...], g_ref[...], RMS_EPS).astype(BF16)
    n_out = o_ref.shape[1]
    for c in range(n_out // ROT_CHUNK):
        cols = slice(c * ROT_CHUNK, (c + 1) * ROT_CHUNK)
        y = _dot(xn, w_ref[:, cols])
        if c * ROT_CHUNK < n_rot:
            y = (y * c_ref[...]
                 + pltpu.roll(y, ROT_CHUNK - half, 1) * s1_ref[...]
                 + pltpu.roll(y, half, 1) * s2_ref[...])
        o_ref[:, cols] = y.astype(BF16)


def _rope_tables(seq, period, rot_dim, theta):
    half = rot_dim // 2
    inv = theta ** (-jnp.arange(0, rot_dim, 2, dtype=F32) / rot_dim)
    ang = jnp.arange(seq, dtype=F32)[:, None] * inv[None, :]
    cos, sin = jnp.cos(ang), jnp.sin(ang)
    rest = period - rot_dim
    one, zero_h, zero_r = jnp.ones((seq, rest), F32), jnp.zeros((seq, half), F32), jnp.zeros((seq, rest), F32)
    reps = ROT_CHUNK // period
    c = jnp.tile(jnp.concatenate([cos, cos, one], -1), (1, reps))
    s1 = jnp.tile(jnp.concatenate([-sin, zero_h, zero_r], -1), (1, reps))
    s2 = jnp.tile(jnp.concatenate([zero_h, sin, zero_r], -1), (1, reps))
    return c, s1, s2


def _norm_proj(h, gain, w, seq, rope=None, n_rot=0, half=0, tm=512):
    n, d = h.shape
    n_out = w.shape[1]
    nt = seq // tm
    in_specs = [pl.BlockSpec((tm, d), lambda i: (i, 0)),
                _resident((1, d), lambda i: (0, 0)),
                _resident((d, n_out), lambda i: (0, 0))]
    args = [h, gain.reshape(1, d), w]
    if n_rot:
        in_specs += [pl.BlockSpec((tm, ROT_CHUNK), lambda i: (i % nt, 0))] * 3
        args += list(rope)
    return pl.pallas_call(
        functools.partial(_norm_proj_kernel, n_rot=n_rot, half=half),
        grid=(n // tm,),
        in_specs=in_specs,
        out_specs=pl.BlockSpec((tm, n_out), lambda i: (i, 0)),
        out_shape=jax.ShapeDtypeStruct((n, n_out), BF16),
        compiler_params=_cparams("parallel"),
        name="norm_proj",
    )(*args)


def _out_proj_kernel(y_ref, w_ref, h_ref, o_ref):
    o_ref[...] = h_ref[...] + _dot(y_ref[...], w_ref[...])


def _out_proj(y, w, h, tm=512):
    n, k = y.shape
    d = w.shape[1]
    return pl.pallas_call(
        _out_proj_kernel,
        grid=(n // tm,),
        in_specs=[pl.BlockSpec((tm, k), lambda i: (i, 0)),
                  _resident((k, d), lambda i: (0, 0)),
                  pl.BlockSpec((tm, d), lambda i: (i, 0))],
        out_specs=pl.BlockSpec((tm, d), lambda i: (i, 0)),
        out_shape=jax.ShapeDtypeStruct((n, d), F32),
        compiler_params=_cparams("parallel"),
        name="out_proj",
    )(y, w, h)


def _ffn_kernel(h_ref, halo_ref, g_ref, wg_ref, wu_ref, cw_ref, cb_ref, wo_ref, gf_ref, o_ref,
                xn_ref, acc_ref, *, nt, final_norm):
    i = pl.program_id(0)
    tm = h_ref.shape[0]
    gain = g_ref[...]
    xn_ref[0:BF16_ROWS, :] = _rms(halo_ref[...], gain, RMS_EPS).astype(BF16)
    xn_ref[BF16_ROWS:, :] = _rms(h_ref[...], gain, RMS_EPS).astype(BF16)
    n_zero = jnp.where(i % nt == 0, BF16_ROWS, 0)
    keep = lax.broadcasted_iota(jnp.int32, (tm + BF16_ROWS, 1), 0) >= n_zero
    acc_ref[...] = jnp.zeros_like(acc_ref)

    def body(j, carry):
        hg = _dot(xn_ref[...], wg_ref[j])
        hg = jnp.where(keep, hg, 0.0)
        cw = cw_ref[j]
        gt = (cw[2:3] * hg + cw[1:2] * pltpu.roll(hg, 1, 0) + cw[0:1] * pltpu.roll(hg, 2, 0))
        gt = gt[BF16_ROWS:] + cb_ref[j]
        up = _dot(xn_ref[BF16_ROWS:, :], wu_ref[j])
        z = (gt * jax.nn.sigmoid(gt) * up).astype(BF16)
        acc_ref[...] += _dot(z, wo_ref[j])
        return carry

    lax.fori_loop(0, wg_ref.shape[0], body, 0)
    out = h_ref[...] + acc_ref[...]
    if final_norm:
        out = _rms(out, gf_ref[...], RMS_EPS)
    o_ref[...] = out


def _ffn(h, gain, w_in, conv_w, conv_b, w_out, final_gain, seq, final_norm, tm=512):
    n, d = h.shape
    d_ff = w_out.shape[0]
    nj = d_ff // FFN_CHUNK
    wg = w_in[:, :d_ff].reshape(d, nj, FFN_CHUNK).transpose(1, 0, 2).astype(BF16)
    wu = w_in[:, d_ff:].reshape(d, nj, FFN_CHUNK).transpose(1, 0, 2).astype(BF16)
    wo = w_out.reshape(nj, FFN_CHUNK, d).astype(BF16)
    cw = conv_w.reshape(FFN_CONV, nj, FFN_CHUNK).transpose(1, 0, 2)
    cb = conv_b.reshape(nj, 1, FFN_CHUNK)
    halo_blocks = tm // BF16_ROWS
    return pl.pallas_call(
        functools.partial(_ffn_kernel, nt=seq // tm, final_norm=final_norm),
        grid=(n // tm,),
        in_specs=[pl.BlockSpec((tm, d), lambda i: (i, 0)),
                  pl.BlockSpec((BF16_ROWS, d), lambda i: (jnp.maximum(i * halo_blocks - 1, 0), 0)),
                  _resident((1, d), lambda i: (0, 0)),
                  _resident((nj, d, FFN_CHUNK), lambda i: (0, 0, 0)),
                  _resident((nj, d, FFN_CHUNK), lambda i: (0, 0, 0)),
                  _resident((nj, FFN_CONV, FFN_CHUNK), lambda i: (0, 0, 0)),
                  _resident((nj, 1, FFN_CHUNK), lambda i: (0, 0, 0)),
                  _resident((nj, FFN_CHUNK, d), lambda i: (0, 0, 0)),
                  _resident((1, d), lambda i: (0, 0))],
        out_specs=pl.BlockSpec((tm, d), lambda i: (i, 0)),
        out_shape=jax.ShapeDtypeStruct((n, d), F32),
        scratch_shapes=[pltpu.VMEM((tm + BF16_ROWS, d), BF16), pltpu.VMEM((tm, d), F32)],
        compiler_params=_cparams("parallel"),
        name="conv_ffn",
    )(h, h, gain.reshape(1, d), wg, wu, cw, cb, wo, final_gain.reshape(1, d))


def _flash_update(s, v, m_ref, acc_ref):
    parts = [s[:, c * LANES:(c + 1) * LANES] for c in range(s.shape[1] // LANES)]
    m_old = m_ref[...]
    m_new = jnp.maximum(m_old, jnp.max(functools.reduce(jnp.maximum, parts), axis=-1, keepdims=True))
    alpha = jnp.exp(m_old - m_new)
    p = jnp.concatenate([jnp.exp(x - m_new) for x in parts], axis=1).astype(BF16)
    v1 = jnp.concatenate([v, jnp.ones_like(v)], axis=1)
    acc_ref[...] = jnp.concatenate([alpha, alpha], axis=1) * acc_ref[...] + _dot(p, v1)
    m_ref[...] = m_new


def _flash_init(m_ref, acc_ref):
    m_ref[...] = jnp.full(m_ref.shape, -jnp.inf, F32)
    acc_ref[...] = jnp.zeros_like(acc_ref)


def _flash_result(acc_ref):
    acc = acc_ref[...]
    return acc[:, :LANES] / acc[:, LANES:]


def _moba_kernel(q_ref, k_ref, v_ref, o_ref, kmean_ref, sel_ref, sa_ref, sb_ref, m_ref, acc_ref, *, nb, scale):
    qi = pl.program_id(2)
    blk, dh = q_ref.shape

    @pl.when(qi == 0)
    def _():
        kf = k_ref[...].astype(F32).reshape(nb, blk, dh)
        kmean_ref[...] = (jnp.sum(kf, axis=1) * (1.0 / blk)).astype(BF16)

    q = q_ref[...]
    gate = _dot_nt(kmean_ref[...], q)
    nidx = lax.broadcasted_iota(jnp.int32, (nb, blk), 0)
    past = nidx < qi
    gate = jnp.where(past, gate, -jnp.inf)
    rank = jnp.zeros((nb, blk), jnp.int32)
    for m in range(nb):
        gm = gate[m:m + 1, :]
        rank = rank + jnp.where(gm == gate, jnp.where(m < nidx, 1, 0), jnp.where(gm > gate, 1, 0))
    sel_t = jnp.where(past, (rank < MOBA_TOPK).astype(F32), 0.0)
    sel_t = jnp.concatenate([sel_t, jnp.zeros((LANES - nb, blk), F32)], axis=0)
    sel_ref[...] = sel_t.T

    def rows(n):
        return pl.ds(pl.multiple_of(n * blk, blk), blk)

    def scores(n):
        return _dot_nt(q, k_ref[rows(n), :]) * scale

    lane = lax.broadcasted_iota(jnp.int32, (blk, LANES), 1)

    def past_update(s_buf, n):
        chosen = jnp.sum(jnp.where(lane == n, sel_ref[...], 0.0), axis=-1, keepdims=True) > 0.5
        _flash_update(jnp.where(chosen, s_buf[...], -jnp.inf), v_ref[rows(n), :], m_ref, acc_ref)

    _flash_init(m_ref, acc_ref)
    s = scores(qi)
    sb_ref[...] = scores(0)
    r_id = lax.broadcasted_iota(jnp.int32, (blk, blk), 0)
    c_id = lax.broadcasted_iota(jnp.int32, (blk, blk), 1)
    _flash_update(jnp.where(c_id <= r_id, s, -jnp.inf), v_ref[rows(qi), :], m_ref, acc_ref)

    def body(it, carry):
        n = 2 * it
        sa_ref[...] = scores(n + 1)
        past_update(sb_ref, n)
        sb_ref[...] = scores(n + 2)
        past_update(sa_ref, n + 1)
        return carry

    lax.fori_loop(0, qi // 2, body, 0)

    @pl.when(qi % 2 == 1)
    def _():
        past_update(sb_ref, qi - 1)

    o_ref[...] = _flash_result(acc_ref).astype(BF16)


def _moba_attention(qkv, batch, seq, heads, dh):
    assert seq % MOBA_BLOCK == 0 and dh == LANES
    nb = seq // MOBA_BLOCK
    assert nb <= LANES and nb % 8 == 0
    blk = MOBA_BLOCK
    return pl.pallas_call(
        functools.partial(_moba_kernel, nb=nb, scale=dh ** -0.5),
        grid=(batch, heads, nb),
        in_specs=[pl.BlockSpec((blk, dh), lambda b, h, i: (b * nb + i, h)),
                  pl.BlockSpec((seq, dh), lambda b, h, i: (b, heads + h)),
                  pl.BlockSpec((seq, dh), lambda b, h, i: (b, 2 * heads + h))],
        out_specs=pl.BlockSpec((blk, dh), lambda b, h, i: (b * nb + i, h)),
        out_shape=jax.ShapeDtypeStruct((batch * seq, heads * dh), BF16),
        scratch_shapes=[pltpu.VMEM((nb, dh), BF16), pltpu.VMEM((blk, LANES), F32),
                        pltpu.VMEM((blk, blk), F32), pltpu.VMEM((blk, blk), F32), pltpu.VMEM((blk, LANES), F32),
                        pltpu.VMEM((blk, 2 * LANES), F32)],
        compiler_params=_cparams("parallel", "parallel", "arbitrary"),
        name="moba_attention",
    )(qkv, qkv, qkv)


def _diff_attn_kernel(q_ref, k_ref, v_ref, lam_ref, g_ref, o_ref, qq_ref, sa_ref, sb_ref, m_ref, acc_ref, *,
                      scale, lam_init):
    qi = pl.program_id(2)
    tq, width = q_ref.shape
    q = q_ref[...] * scale
    lane = lax.broadcasted_iota(jnp.int32, (tq, width), 1)
    zero = jnp.zeros_like(q)
    qq_ref[0:tq, :] = jnp.where(lane < width // 2, q, zero)
    qq_ref[tq:, :] = jnp.where(lane >= width // 2, q, zero)

    def rows(n):
        return pl.ds(pl.multiple_of(n * tq, tq), tq)

    def scores(n):
        return _dot_nt(qq_ref[...], k_ref[rows(n), :])

    def diag_update(s_buf):
        r_id = lax.broadcasted_iota(jnp.int32, (2 * tq, tq), 0)
        r_id = jnp.where(r_id >= tq, r_id - tq, r_id)
        c_id = lax.broadcasted_iota(jnp.int32, (2 * tq, tq), 1)
        _flash_update(jnp.where(c_id <= r_id, s_buf[...], -jnp.inf), v_ref[rows(qi), :], m_ref, acc_ref)

    _flash_init(m_ref, acc_ref)
    sa_ref[...] = scores(0)

    def body(it, carry):
        n = 2 * it
        sb_ref[...] = scores(n + 1)
        _flash_update(sa_ref[...], v_ref[rows(n), :], m_ref, acc_ref)
        sa_ref[...] = scores(n + 2)
        _flash_update(sb_ref[...], v_ref[rows(n + 1), :], m_ref, acc_ref)
        return carry

    lax.fori_loop(0, qi // 2, body, 0)

    @pl.when(qi % 2 == 0)
    def _():
        diag_update(sa_ref)

    @pl.when(qi % 2 == 1)
    def _():
        sb_ref[...] = scores(qi)
        _flash_update(sa_ref[...], v_ref[rows(qi - 1), :], m_ref, acc_ref)
        diag_update(sb_ref)

    lv = lam_ref[...]
    lam = (jnp.exp(jnp.sum(lv[0:1] * lv[1:2], axis=-1, keepdims=True))
           - jnp.exp(jnp.sum(lv[2:3] * lv[3:4], axis=-1, keepdims=True)) + lam_init)
    o = _flash_result(acc_ref)
    o = o[:tq] - lam * o[tq:]
    o_ref[...] = (_rms(o, g_ref[...], SUBLN_EPS) * (1.0 - lam_init)).astype(BF16)


def _diff_attention(qkv, lam_vecs, subln_g, batch, seq, heads, lam_init, tq=256):
    width = qkv.shape[1] // (3 * heads)
    nq = seq // tq
    dh = width // 2
    return pl.pallas_call(
        functools.partial(_diff_attn_kernel, scale=dh ** -0.5, lam_init=lam_init),
        grid=(batch, heads, nq),
        in_specs=[pl.BlockSpec((tq, width), lambda b, h, i: (b * nq + i, h)),
                  pl.BlockSpec((seq, width), lambda b, h, i: (b, heads + h)),
                  pl.BlockSpec((seq, width), lambda b, h, i: (b, 2 * heads + h)),
                  _resident(lam_vecs.shape, lambda b, h, i: (0, 0)),
                  _resident((1, width), lambda b, h, i: (0, 0))],
        out_specs=pl.BlockSpec((tq, width), lambda b, h, i: (b * nq + i, h)),
        out_shape=jax.ShapeDtypeStruct((batch * seq, heads * width), BF16),
        scratch_shapes=[pltpu.VMEM((2 * tq, width), BF16), pltpu.VMEM((2 * tq, tq), F32), pltpu.VMEM((2 * tq, tq), F32),
                        pltpu.VMEM((2 * tq, LANES), F32), pltpu.VMEM((2 * tq, 2 * LANES), F32)],
        compiler_params=_cparams("parallel", "parallel", "arbitrary"),
        name="diff_attention",
    )(qkv, qkv, qkv, lam_vecs, subln_g.reshape(1, width))


def _rglru_kernel(gate_ref, x_ref, cw_ref, cb_ref, wa_ref, ba_ref, wx_ref, bx_ref, lam_ref, o_ref,
                  prev_ref, hprev_ref, *, conv_width):
    t = pl.program_id(2)
    ts, tc = x_ref.shape
    halo = prev_ref.shape[0]

    @pl.when(t == 0)
    def _():
        prev_ref[...] = jnp.zeros_like(prev_ref)
        hprev_ref[...] = jnp.zeros_like(hprev_ref)

    x = x_ref[...].astype(F32)
    full = jnp.concatenate([prev_ref[...], x], axis=0)
    prev_ref[...] = x[ts - halo:, :]
    cw = cw_ref[...]
    rec = cw[conv_width - 1:conv_width] * full
    for k in range(1, conv_width):
        rec = rec + cw[conv_width - 1 - k:conv_width - k] * pltpu.roll(full, k, 0)
    rec = rec[halo:] + cb_ref[...]

    rec_bf = rec.astype(BF16)
    r_parts, i_parts = [], []
    for g in range(tc // C_BLOCK_DIM):
        cols = slice(g * C_BLOCK_DIM, (g + 1) * C_BLOCK_DIM)
        r_parts.append(_dot(rec_bf[:, cols], wa_ref[g]))
        i_parts.append(_dot(rec_bf[:, cols], wx_ref[g]))
    r = jax.nn.sigmoid(jnp.concatenate(r_parts, axis=-1) + ba_ref[...])
    ig = jax.nn.sigmoid(jnp.concatenate(i_parts, axis=-1) + bx_ref[...])

    softplus = jnp.log1p(jnp.exp(-lam_ref[...]))
    log_a = (-LRU_C) * r * softplus
    a = jnp.exp(log_a)
    b = jnp.sqrt(1.0 - a * a) * (ig * rec)

    row = lax.broadcasted_iota(jnp.int32, (ts, 1), 0)
    d = 1
    while d < ts:
        ok = row >= d
        b = jnp.where(ok, a * pltpu.roll(b, d, 0) + b, b)
        a = jnp.where(ok, a * pltpu.roll(a, d, 0), a)
        d *= 2
    h = a * hprev_ref[...] + b
    hprev_ref[...] = h[ts - 1:ts, :]
    o_ref[...] = (h * jax.nn.gelu(gate_ref[...].astype(F32))).astype(BF16)


def _rglru(xy, conv_w, conv_b, w_a, b_a, w_x, b_x, lam, batch, seq, ts=256, tc=256):
    width = xy.shape[1] // 2
    nt, ncb = seq // ts, width // tc
    gpb = tc // C_BLOCK_DIM
    conv_width = conv_w.shape[0]
    vec = lambda v: v.reshape(1, width)
    vspec = pl.BlockSpec((1, tc), lambda b, c, t: (0, c))
    wspec = pl.BlockSpec((gpb, C_BLOCK_DIM, C_BLOCK_DIM), lambda b, c, t: (c, 0, 0))
    return pl.pallas_call(
        functools.partial(_rglru_kernel, conv_width=conv_width),
        grid=(batch, ncb, nt),
        in_specs=[pl.BlockSpec((ts, tc), lambda b, c, t: (b * nt + t, c)),
                  pl.BlockSpec((ts, tc), lambda b, c, t: (b * nt + t, ncb + c)),
                  pl.BlockSpec((conv_width, tc), lambda b, c, t: (0, c)),
                  vspec, wspec, vspec, wspec, vspec, vspec],
        out_specs=pl.BlockSpec((ts, tc), lambda b, c, t: (b * nt + t, c)),
        out_shape=jax.ShapeDtypeStruct((batch * seq, width), BF16),
        scratch_shapes=[pltpu.VMEM((8, tc), F32), pltpu.VMEM((1, tc), F32)],
        compiler_params=_cparams("parallel", "parallel", "arbitrary"),
        name="rglru",
    )(xy, xy, conv_w, vec(conv_b), w_a.astype(BF16), vec(b_a), w_x.astype(BF16), vec(b_x), vec(lam))


def _retention_kernel(q_ref, k_ref, v_ref, g_ref, gn_ref, o_ref, state_ref, decay_ref, *, k_scale):
    hd = pl.program_id(1)
    c = pl.program_id(2)
    ch = q_ref.shape[0]

    log_gamma = jnp.log1p(-jnp.exp2(-5.0 - jnp.full((1, 1), hd, jnp.int32).astype(F32)))

    @pl.when(c == 0)
    def _():
        state_ref[...] = jnp.zeros_like(state_ref)
        r_id = lax.broadcasted_iota(jnp.int32, (ch, ch), 0)
        c_id = lax.broadcasted_iota(jnp.int32, (ch, ch), 1)
        diff = (r_id - c_id).astype(F32)
        decay_ref[...] = jnp.where(diff >= 0, jnp.exp(jnp.maximum(diff, 0.0) * log_gamma), 0.0)

    pos = lax.broadcasted_iota(jnp.int32, (ch, 1), 0).astype(F32)
    xi = jnp.exp((pos + 1.0) * log_gamma)
    zeta = jnp.exp((ch - 1.0 - pos) * log_gamma)
    gamma_chunk = jnp.exp(ch * log_gamma)

    q = q_ref[...]
    k = k_ref[...].astype(F32) * k_scale
    v = v_ref[...]
    inner = _dot_nt(q, k.astype(BF16)) * decay_ref[...]
    state = state_ref[...]
    o = _dot(inner.astype(BF16), v) + _dot(q, state.astype(BF16)) * xi
    state_ref[...] = gamma_chunk * state + _dot_tn((k * zeta).astype(BF16), v)

    mu = jnp.mean(o, axis=-1, keepdims=True)
    var = jnp.mean(jnp.square(o - mu), axis=-1, keepdims=True)
    o = (o - mu) * lax.rsqrt(var + GN_EPS) * gn_ref[...]
    g = g_ref[...].astype(F32)
    o_ref[...] = (g * jax.nn.sigmoid(g) * o).astype(BF16)


def _retention(proj, gn_g, batch, seq, heads, dk, dv, chunk=256):
    nc = seq // chunk
    kb, vb, gb = heads, (2 * heads * dk) // dv, (2 * heads * dk) // dv + heads
    return pl.pallas_call(
        functools.partial(_retention_kernel, k_scale=dk ** -0.5),
        grid=(batch, heads, nc),
        in_specs=[pl.BlockSpec((chunk, dk), lambda b, h, c: (b * nc + c, h)),
                  pl.BlockSpec((chunk, dk), lambda b, h, c: (b * nc + c, kb + h)),
                  pl.BlockSpec((chunk, dv), lambda b, h, c: (b * nc + c, vb + h)),
                  pl.BlockSpec((chunk, dv), lambda b, h, c: (b * nc + c, gb + h)),
                  pl.BlockSpec((1, dv), lambda b, h, c: (0, h))],
        out_specs=pl.BlockSpec((chunk, dv), lambda b, h, c: (b * nc + c, h)),
        out_shape=jax.ShapeDtypeStruct((batch * seq, heads * dv), BF16),
        scratch_shapes=[pltpu.VMEM((dk, dv), F32), pltpu.VMEM((chunk, chunk), F32)],
        compiler_params=_cparams("parallel", "parallel", "arbitrary"),
        name="retention",
    )(proj, proj, proj, proj, gn_g.reshape(1, heads * dv))


def kernel(x, norm_mix_g, norm_ffn_g, norm_final_g, a_w_in, a_w_out, b_w_in, b_w_out, b_lam_q1, b_lam_k1,
           b_lam_q2, b_lam_k2, b_subln_g, c_w_in, c_conv_w, c_conv_b, c_w_a, c_b_a, c_w_x, c_b_x, c_lambda,
           c_w_out, d_w_in, d_gn_g, d_w_out, ffn_w_in, ffn_conv_w, ffn_conv_b, ffn_w_out):
    batch, seq, d = x.shape
    depth = norm_mix_g.shape[0]
    n_mixers = 4
    h = x.reshape(batch * seq, d)
    for i in range(depth):
        m, j = i % n_mixers, i // n_mixers
        if m == 0:
            dh = d // A_HEADS
            rope = _rope_tables(seq, dh, dh // 4, ROPE_THETA)
            qkv = _norm_proj(h, norm_mix_g[i], a_w_in[j].astype(BF16), seq, rope, n_rot=2 * d, half=dh // 8)
            y = _moba_attention(qkv, batch, seq, A_HEADS, dh)
            h = _out_proj(y, a_w_out[j].astype(BF16), h)
        elif m == 1:
            dh = d // (2 * B_HEADS)
            rope = _rope_tables(seq, dh, dh // 4, ROPE_THETA)
            qkv = _norm_proj(h, norm_mix_g[i], b_w_in[j].astype(BF16), seq, rope, n_rot=2 * d, half=dh // 8)
            lam_vecs = jnp.stack([b_lam_q1[j], b_lam_k1[j], b_lam_q2[j], b_lam_k2[j]])
            lam_init = 0.8 - 0.6 * math.exp(-0.3 * i)
            y = _diff_attention(qkv, lam_vecs, b_subln_g[j], batch, seq, B_HEADS, lam_init)
            h = _out_proj(y, b_w_out[j].astype(BF16), h)
        elif m == 2:
            xy = _norm_proj(h, norm_mix_g[i], c_w_in[j].astype(BF16), seq)
            y = _rglru(xy, c_conv_w[j], c_conv_b[j], c_w_a[j], c_b_a[j], c_w_x[j], c_b_x[j], c_lambda[j],
                       batch, seq)
            h = _out_proj(y, c_w_out[j].astype(BF16), h)
        else:
            dk = d // D_HEADS
            rope = _rope_tables(seq, dk, dk, RET_THETA)
            proj = _norm_proj(h, norm_mix_g[i], d_w_in[j].astype(BF16), seq, rope, n_rot=2 * D_HEADS * dk,
                              half=dk // 2)
            y = _retention(proj, d_gn_g[j], batch, seq, D_HEADS, dk, 2 * dk)
            h = _out_proj(y, d_w_out[j].astype(BF16), h)
        h = _ffn(h, norm_ffn_g[i], ffn_w_in[i], ffn_conv_w[i], ffn_conv_b[i], ffn_w_out[i], norm_final_g, seq,
                 final_norm=(i == depth - 1))
    return h.reshape(batch, seq, d)
```
